```python
import jax, jax.numpy as jnp
from jax import lax
import numpy as np

D_MODEL = 2048
BATCH = 2
SEQ = 8192
DEPTH = 2

N_A_LAYERS = DEPTH // 2
N_B_LAYERS = DEPTH - N_A_LAYERS

HG_EXPAND = 128
HG_HEADS = D_MODEL // HG_EXPAND
HG_DK = HG_EXPAND
HG_DV = D_MODEL // HG_HEADS
HG_CHUNK = 64

ATT_HEADS = 16
ATT_HD = D_MODEL // ATT_HEADS
MOBA_BLOCK = 256
MOBA_TOPK = 3
MOBA_QCHUNK = 16
ROPE_THETA = 10000.0

PEER_HEADS = 8
PEER_NKEYS = 128
PEER_EXPERTS = PEER_NKEYS * PEER_NKEYS
PEER_DKEY = 256
PEER_TOPK = 16
PEER_CHUNK = 128

ALPHA = (2.0 * DEPTH) ** 0.25
BETA = (8.0 * DEPTH) ** -0.25

LN_EPS = 1e-5
RMS_EPS = 1e-6
NEG = -1e30
F32 = jnp.float32

kernel_name = 'hgrn2_moba_peer_yoco_deepnorm'


def layer_norm(x, g, b):
    xf = x.astype(F32)
    mu = xf.mean(-1, keepdims=True)
    var = jnp.square(xf - mu).mean(-1, keepdims=True)
    return ((xf - mu) * lax.rsqrt(var + LN_EPS) * g + b).astype(x.dtype)


def rotary(t, pos):
    half = t.shape[-1] // 2
    inv = ROPE_THETA ** (-jnp.arange(half, dtype=F32) / half)
    ang = pos.astype(F32)[:, None] * inv[None, :]
    cos = jnp.cos(ang).astype(t.dtype)
    sin = jnp.sin(ang).astype(t.dtype)
    t1, t2 = t[..., :half], t[..., half:]
    return jnp.concatenate([t1 * cos - t2 * sin, t1 * sin + t2 * cos], axis=-1)


def hgrn2_mixer(x, w_in, g_norm, w_out, lb):
    B, T, _ = x.shape
    H, dk, dv, C = HG_HEADS, HG_DK, HG_DV, HG_CHUNK
    nc = T // C
    q, fz, i, g = jnp.split(x @ w_in, 4, axis=-1)
    q = jax.nn.silu(q.astype(F32))
    f = lb + (1.0 - lb) * jax.nn.sigmoid(fz.astype(F32))
    logf = jnp.log(f)
    k = 1.0 - f

    def to_chunks(t, d):
        return t.reshape(B, nc, C, H, d).transpose(1, 0, 3, 2, 4)

    qc, kc, lc = to_chunks(q, dk), to_chunks(k, dk), to_chunks(logf, dk)
    ic = to_chunks(i.astype(F32), dv)
    causal = jnp.tril(jnp.ones((C, C), dtype=bool))

    def step(S, inp):
        qj, kj, ij, lj = inp
        b = jnp.cumsum(lj, axis=2)
        inter = jnp.einsum('bhtk,bhkv->bhtv', qj * jnp.exp(b), S)
        diff = b[:, :, :, None, :] - b[:, :, None, :, :]
        decay = jnp.where(causal[:, :, None], jnp.exp(jnp.minimum(diff, 0.0)), 0.0)
        scores = jnp.einsum('bhtk,bhsk,bhtsk->bhts', qj, kj, decay)
        intra = jnp.einsum('bhts,bhsv->bhtv', scores, ij)
        b_last = b[:, :, -1:, :]
        S = jnp.exp(b_last[:, :, 0, :, None]) * S + jnp.einsum(
            'bhsk,bhsv->bhkv', kj * jnp.exp(b_last - b), ij)
        return S, inter + intra

    S0 = jnp.zeros((B, H, dk, dv), F32)
    _, o = lax.scan(step, S0, (qc, kc, ic, lc))
    o = o.transpose(1, 0, 3, 2, 4).reshape(B, T, H, dv)
    o = o * lax.rsqrt(jnp.mean(o * o, axis=-1, keepdims=True) + RMS_EPS) * g_norm
    o = o * jax.nn.silu(g.astype(F32)).reshape(B, T, H, dv)
    return o.reshape(B, T, H * dv).astype(x.dtype) @ w_out


def shared_kv(x, w_kv):
    B, T, _ = x.shape
    H, hd, blk = ATT_HEADS, ATT_HD, MOBA_BLOCK
    nb = -(-T // blk)
    t_pad = nb * blk
    k, v = jnp.split(x @ w_kv, 2, axis=-1)
    k = k.reshape(B, T, H, hd).transpose(0, 2, 1, 3)
    v = v.reshape(B, T, H, hd).transpose(0, 2, 1, 3)
    k = rotary(k, jnp.arange(T))
    pad = ((0, 0), (0, 0), (0, t_pad - T), (0, 0))
    k_blk = jnp.pad(k, pad).reshape(B, H, nb, blk, hd)
    v_blk = jnp.pad(v, pad).reshape(B, H, nb, blk, hd)
    k_mean = k_blk.astype(F32).mean(axis=3)
    return k_blk, v_blk, k_mean


def moba_mixer(x, w_q, w_out, k_blk, v_blk, k_mean):
    B, T, _ = x.shape
    H, hd, blk, K, qcs = ATT_HEADS, ATT_HD, MOBA_BLOCK, MOBA_TOPK, MOBA_QCHUNK
    nb = k_blk.shape[2]
    nq = T // qcs
    pos = jnp.arange(T)
    q = (x @ w_q).reshape(B, T, H, hd).transpose(0, 2, 1, 3)
    q = rotary(q, pos)
    gate = jnp.einsum('bhtd,bhnd->bhtn', q.astype(F32), k_mean)
    past = jnp.arange(nb)[None, :] < (pos // blk)[:, None]
    gate = jnp.where(past, gate, NEG)
    if nb < K:
        gate = jnp.pad(gate, ((0, 0), (0, 0), (0, 0), (0, K - nb)), constant_values=NEG)
    top_val, top_idx = lax.top_k(gate, K)
    sel_ok = top_val > 0.5 * NEG
    top_idx = jnp.minimum(top_idx, nb - 1)

    def chunks(t):
        return jnp.moveaxis(t.reshape(B, H, nq, qcs, *t.shape[3:]), 2, 0)

    bi = jnp.arange(B)[:, None, None, None]
    hi = jnp.arange(H)[None, :, None, None]
    scale = hd ** -0.5

    def attend(args):
        c, qj, idx, ok = args
        start = c * qcs
        blk_id = start // blk
        k_sel = k_blk[bi, hi, idx]
        v_sel = v_blk[bi, hi, idx]
        s_sel = jnp.einsum('bhqd,bhqjkd->bhqjk', qj, k_sel).astype(F32) * scale
        s_sel = jnp.where(ok[..., None], s_sel, NEG)
        k_own = lax.dynamic_index_in_dim(k_blk, blk_id, axis=2, keepdims=False)
        v_own = lax.dynamic_index_in_dim(v_blk, blk_id, axis=2, keepdims=False)
        s_own = jnp.einsum('bhqd,bhkd->bhqk', qj, k_own).astype(F32) * scale
        q_pos = start + jnp.arange(qcs)
        k_pos = blk_id * blk + jnp.arange(blk)
        s_own = jnp.where(k_pos[None, :] <= q_pos[:, None], s_own, NEG)
        p = jax.nn.softmax(jnp.concatenate([s_sel.reshape(B, H, qcs, K * blk), s_own], axis=-1),
                           axis=-1).astype(qj.dtype)
        p_sel = p[..., :K * blk].reshape(B, H, qcs, K, blk)
        p_own = p[..., K * blk:]
        return (jnp.einsum('bhqjk,bhqjkd->bhqd', p_sel, v_sel)
                + jnp.einsum('bhqk,bhkd->bhqd', p_own, v_own))

    o = lax.map(attend, (jnp.arange(nq), chunks(q), chunks(top_idx), chunks(sel_ok)))
    o = jnp.moveaxis(o, 0, 2).reshape(B, H, T, hd).transpose(0, 2, 1, 3).reshape(B, T, H * hd)
    return o @ w_out


def peer_ffn(x, w_query, sub_keys, u, v):
    B, T, D = x.shape
    n = B * T
    xt = x.reshape(n, D)
    q = (xt @ w_query).reshape(n, PEER_HEADS, PEER_DKEY)
    half = PEER_DKEY // 2
    s1 = jnp.einsum('thd,hnd->thn', q[..., :half], sub_keys[0]).astype(F32)
    s2 = jnp.einsum('thd,hnd->thn', q[..., half:], sub_keys[1]).astype(F32)
    v1, i1 = lax.top_k(s1, PEER_TOPK)
    v2, i2 = lax.top_k(s2, PEER_TOPK)
    cand = (v1[..., :, None] + v2[..., None, :]).reshape(n, PEER_HEADS, PEER_TOPK * PEER_TOPK)
    best, flat = lax.top_k(cand, PEER_TOPK)
    expert = (jnp.take_along_axis(i1, flat // PEER_TOPK, axis=-1) * PEER_NKEYS
              + jnp.take_along_axis(i2, flat % PEER_TOPK, axis=-1))
    gate = jax.nn.softmax(best, axis=-1).astype(x.dtype)
    n_chunk = n // PEER_CHUNK
    kk = PEER_HEADS * PEER_TOPK

    def experts(args):
        xc, ec, gc = args
        hidden = jax.nn.gelu(jnp.einsum('td,tkd->tk', xc, u[ec]), approximate=False)
        return jnp.einsum('tk,tkd->td', gc * hidden, v[ec])

    y = lax.map(experts, (xt.reshape(n_chunk, PEER_CHUNK, D),
                          expert.reshape(n_chunk, PEER_CHUNK, kk),
                          gate.reshape(n_chunk, PEER_CHUNK, kk)))
    return y.reshape(B, T, D)


def setup_inputs(seed: int = 0) -> dict:
    key = jax.random.key(seed)
    ks = jax.random.split(key, 14)
    D = D_MODEL
    s = D ** -0.5
    hg_cols = jnp.concatenate([jnp.ones((2 * D,), F32), jnp.full((D,), BETA, F32), jnp.ones((D,), F32)])
    kv_cols = jnp.concatenate([jnp.ones((D,), F32), jnp.full((D,), BETA, F32)])
    return {
        'x': jax.random.normal(ks[0], (BATCH, SEQ, D), F32),
        'hg_w_in': jax.random.normal(ks[1], (N_A_LAYERS, D, 4 * D), F32) * s * hg_cols,
        'hg_norm_g': 1.0 + 0.02 * jax.random.normal(ks[2], (N_A_LAYERS, HG_DV), F32),
        'hg_w_out': jax.random.normal(ks[3], (N_A_LAYERS, D, D), F32) * s * BETA,
        'hg_lb_logits': 0.1 * jax.random.normal(ks[4], (N_A_LAYERS + 1, D), F32),
        'kv_w': jax.random.normal(ks[5], (D, 2 * D), F32) * s * kv_cols,
        'moba_w_q': jax.random.normal(ks[6], (N_B_LAYERS, D, D), F32) * s,
        'moba_w_out': jax.random.normal(ks[7], (N_B_LAYERS, D, D), F32) * s * BETA,
        'peer_w_query': jax.random.normal(ks[8], (DEPTH, D, PEER_HEADS * PEER_DKEY), F32) * s,
        'peer_sub_keys': jax.random.normal(ks[9], (DEPTH, 2, PEER_HEADS, PEER_NKEYS, PEER_DKEY // 2), F32)
                         * (PEER_DKEY // 2) ** -0.5,
        'peer_u': jax.random.normal(ks[10], (DEPTH, PEER_EXPERTS, D), F32) * s,
        'peer_v': jax.random.normal(ks[11], (DEPTH, PEER_EXPERTS, D), F32) * BETA,
        'ln_g': 1.0 + 0.02 * jax.random.normal(ks[12], (DEPTH, 2, D), F32),
        'ln_b': 0.02 * jax.random.normal(ks[13], (DEPTH, 2, D), F32),
    }


def reference(x, hg_w_in, hg_norm_g, hg_w_out, hg_lb_logits, kv_w, moba_w_q, moba_w_out,
              peer_w_query, peer_sub_keys, peer_u, peer_v, ln_g, ln_b):
    lower_bounds = jnp.cumsum(jax.nn.softmax(hg_lb_logits.astype(F32), axis=0), axis=0)
    kv = None
    for layer in range(DEPTH):
        if layer < N_A_LAYERS:
            a = layer
            mixed = hgrn2_mixer(x, hg_w_in[a], hg_norm_g[a], hg_w_out[a], lower_bounds[a])
        else:
            b = layer - N_A_LAYERS
            k_blk, v_blk, k_mean = kv
            mixed = moba_mixer(x, moba_w_q[b], moba_w_out[b], k_blk, v_blk, k_mean)
        x = layer_norm(ALPHA * x + mixed, ln_g[layer, 0], ln_b[layer, 0])
        ffn = peer_ffn(x, peer_w_query[layer], peer_sub_keys[layer], peer_u[layer], peer_v[layer])
        x = layer_norm(ALPHA * x + ffn, ln_g[layer, 1], ln_b[layer, 1])
        if layer == N_A_LAYERS - 1:
            kv = shared_kv(x, kv_w)
    return x
```

```python
import functools
import math

import jax
import jax.numpy as jnp
from jax import lax
from jax.experimental import pallas as pl
from jax.experimental.pallas import tpu as pltpu

F32 = jnp.float32
BF16 = jnp.bfloat16

HD = 128
DEPTH = 2
HG_CHUNK = 64
MOBA_BLOCK = 256
MOBA_TOPK = 3
ROPE_THETA = 10000.0
PEER_HEADS = 8
PEER_NKEYS = 128
PEER_TOPK = 16
ALPHA = (2.0 * DEPTH) ** 0.25
LN_EPS = 1e-5
RMS_EPS = 1e-6
NEG = -1e30
VMEM_LIMIT = 56 * 1024 * 1024

_PEER_PAIRS = [(a, b) for a in range(PEER_TOPK) for b in range(PEER_TOPK)
               if (a + 1) * (b + 1) <= PEER_TOPK]
_PEER_CAND_ROWS = -(-len(_PEER_PAIRS) // 8) * 8


def _params(*sem):
    return pltpu.CompilerParams(dimension_semantics=sem, vmem_limit_bytes=VMEM_LIMIT)


def _dot(a, b):
    return jnp.dot(a, b, preferred_element_type=F32)


def _dot_nt(a, b):
    return lax.dot_general(a, b, (((1,), (1,)), ((), ())), preferred_element_type=F32)


def _dot_tn(a, b):
    return lax.dot_general(a, b, (((0,), (0,)), ((), ())), preferred_element_type=F32)


def _split(a):
    hi = a.astype(BF16)
    lo = (a - hi.astype(F32)).astype(BF16)
    return hi, lo


def _dot3(a, b, dot=_dot):
    ah, al = _split(a)
    bh, bl = _split(b)
    return dot(ah, bh) + (dot(ah, bl) + dot(al, bh))


def _layer_norm(z, g, b):
    mu = jnp.mean(z, axis=-1, keepdims=True)
    zc = z - mu
    var = jnp.mean(zc * zc, axis=-1, keepdims=True)
    return zc * lax.rsqrt(var + LN_EPS) * g + b


def _tile(n, want):
    t = min(n, want)
    assert n % t == 0, (n, want)
    return t


def _hg_in_kernel(x_ref, wq_ref, wf_ref, wi_ref, wg_ref, lb_ref,
                  q_ref, lf_ref, k_ref, i_ref, g_ref):
    x = x_ref[...]
    heads = q_ref.shape[0]

    def put(ref, val):
        for h in range(heads):
            ref[h] = val[:, h * HD:(h + 1) * HD].astype(ref.dtype)

    q = _dot(x, wq_ref[...])
    put(q_ref, q * jax.nn.sigmoid(q))
    lb = lb_ref[...]
    f = lb + (1.0 - lb) * jax.nn.sigmoid(_dot(x, wf_ref[...]))
    put(lf_ref, jnp.log(f))
    put(k_ref, 1.0 - f)
    put(i_ref, _dot(x, wi_ref[...]))
    g = _dot(x, wg_ref[...])
    put(g_ref, g * jax.nn.sigmoid(g))


def _hg_in(xb, w_in, lb):
    n, d = xb.shape
    tm = _tile(n, 1024)
    tn = _tile(d, 256)
    nh = d // HD
    hp = tn // HD
    nj = d // tn
    w_specs = [pl.BlockSpec((d, tn), functools.partial(lambda i, j, g: (0, j + g * nj), g=g))
               for g in range(4)]
    hm = lambda dt: jax.ShapeDtypeStruct((nh, n, HD), dt)
    o_spec = pl.BlockSpec((hp, tm, HD), lambda i, j: (j, i, 0))
    return pl.pallas_call(
        _hg_in_kernel,
        grid=(n // tm, nj),
        in_specs=[pl.BlockSpec((tm, d), lambda i, j: (i, 0))] + w_specs
                 + [pl.BlockSpec((1, tn), lambda i, j: (0, j))],
        out_specs=[o_spec] * 5,
        out_shape=[hm(BF16), hm(F32), hm(BF16), hm(BF16), hm(BF16)],
        compiler_params=_params("parallel", "parallel"),
        name="hg_in",
    )(xb, w_in, w_in, w_in, w_in, lb)


def _hg_scan_kernel(q_ref, lf_ref, k_ref, i_ref, g_ref, gn_ref, o_ref, st_ref, *, chunk):
    @pl.when(pl.program_id(2) == 0)
    def _():
        st_ref[...] = jnp.zeros_like(st_ref)

    heads, tb, _ = q_ref.shape
    row = lax.broadcasted_iota(jnp.int32, (chunk, chunk), 0)
    col = lax.broadcasted_iota(jnp.int32, (chunk, chunk), 1)
    causal = row >= col
    tri = causal.astype(BF16)
    gn = gn_ref[...]

    def body(c, carry):
        r0 = pl.multiple_of(c * chunk, chunk)
        rows = pl.ds(r0, chunk)
        for h in range(heads):
            lf = lf_ref[h, rows, :]
            lf_hi, lf_lo = _split(lf)
            b = _dot(tri, lf_hi) + _dot(tri, lf_lo)
            eb = jnp.exp(b)
            enb = jnp.exp(jnp.minimum(-b, 80.0))
            qd = (q_ref[h, rows, :].astype(F32) * eb).astype(BF16)
            kd = k_ref[h, rows, :].astype(F32) * enb
            e_last = eb[chunk - 1:chunk, :]
            kl = (kd * e_last).astype(BF16)
            iv = i_ref[h, rows, :]
            st = st_ref[h]
            inter = _dot_nt(qd, st.astype(BF16))
            att = jnp.where(causal, _dot_nt(qd, kd.astype(BF16)), 0.0)
            o = inter + _dot(att.astype(BF16), iv)
            st_ref[h] = st * e_last + _dot_tn(iv, kl)
            o = o * lax.rsqrt(jnp.mean(o * o, axis=-1, keepdims=True) + RMS_EPS) * gn
            o = o * g_ref[h, rows, :].astype(F32)
            o_ref[rows, h * HD:(h + 1) * HD] = o.astype(o_ref.dtype)
        return carry

    lax.fori_loop(0, tb // chunk, body, 0)


def _hg_scan(q, lf, k, iv, g, g_norm, batch):
    nh, n, _ = q.shape
    t = n // batch
    tb = _tile(t, 1024)
    hp = 2 if nh % 2 == 0 else 1
    nt = t // tb
    spec = pl.BlockSpec((hp, tb, HD), lambda h, b, s: (h, b * nt + s, 0))
    return pl.pallas_call(
        functools.partial(_hg_scan_kernel, chunk=HG_CHUNK),
        grid=(nh // hp, batch, nt),
        in_specs=[spec] * 5 + [pl.BlockSpec((1, HD), lambda h, b, s: (0, 0))],
        out_specs=pl.BlockSpec((tb, hp * HD), lambda h, b, s: (b * nt + s, h)),
        out_shape=jax.ShapeDtypeStruct((n, nh * HD), BF16),
        scratch_shapes=[pltpu.VMEM((hp, HD, HD), F32)],
        compiler_params=_params("parallel", "parallel", "arbitrary"),
        name="hg_scan",
    )(q, lf, k, iv, g, g_norm)


def _proj_ln_kernel(a_ref, w_ref, res_ref, g_ref, b_ref, x_ref, xb_ref):
    z = ALPHA * res_ref[...] + _dot(a_ref[...], w_ref[...])
    out = _layer_norm(z, g_ref[...], b_ref[...])
    x_ref[...] = out
    xb_ref[...] = out.astype(BF16)


def _proj_ln(a, w, res, g, b):
    n, d = res.shape
    kd = a.shape[1]
    tm = _tile(n, 256)
    row = pl.BlockSpec((tm, d), lambda i: (i, 0))
    vec = pl.BlockSpec((1, d), lambda i: (0, 0))
    return pl.pallas_call(
        _proj_ln_kernel,
        grid=(n // tm,),
        in_specs=[pl.BlockSpec((tm, kd), lambda i: (i, 0)),
                  pl.BlockSpec((kd, d), lambda i: (0, 0)), row, vec, vec],
        out_specs=[row, row],
        out_shape=[jax.ShapeDtypeStruct((n, d), F32), jax.ShapeDtypeStruct((n, d), BF16)],
        compiler_params=_params("parallel"),
        name="proj_ln",
    )(a, w, res, g, b)


def _add_ln_kernel(y_ref, res_ref, g_ref, b_ref, x_ref, xb_ref):
    out = _layer_norm(ALPHA * res_ref[...] + y_ref[...], g_ref[...], b_ref[...])
    x_ref[...] = out
    xb_ref[...] = out.astype(BF16)


def _add_ln(y, res, g, b):
    n, d = res.shape
    tm = _tile(n, 512)
    row = pl.BlockSpec((tm, d), lambda i: (i, 0))
    vec = pl.BlockSpec((1, d), lambda i: (0, 0))
    return pl.pallas_call(
        _add_ln_kernel,
        grid=(n // tm,),
        in_specs=[row, row, vec, vec],
        out_specs=[row, row],
        out_shape=[jax.ShapeDtypeStruct((n, d), F32), jax.ShapeDtypeStruct((n, d), BF16)],
        compiler_params=_params("parallel"),
        name="add_ln",
    )(y, res, g, b)


def _top_values(s, k):
    vals = []
    for _ in range(k):
        m = jnp.max(s, axis=0, keepdims=True)
        vals.append(m)
        s = jnp.where(s == m, -jnp.inf, s)
    return vals


def _peer_score_kernel(xT_ref, wq_ref, k1_ref, k2_ref, a1_ref, a2_ref, thr_ref,
                       q_scr, cand_scr):
    q_scr[...] = _dot(wq_ref[...], xT_ref[...])
    tn = q_scr.shape[1]
    half = PEER_NKEYS
    pad = _PEER_CAND_ROWS - len(_PEER_PAIRS)

    def candidates(v1, v2):
        for r, (a, b) in enumerate(_PEER_PAIRS):
            cand_scr[r:r + 1, :] = v1[a] + v2[b]
        if pad:
            cand_scr[len(_PEER_PAIRS):, :] = jnp.full((pad, tn), -jnp.inf, F32)
        return _top_values(cand_scr[...], PEER_TOPK)

    def body(h, carry):
        base = pl.multiple_of(h * 2 * half, 2 * half)
        s1 = _dot3(k1_ref[h], q_scr[pl.ds(base, half), :])
        s2 = _dot3(k2_ref[h], q_scr[pl.ds(base + half, half), :])
        v1 = _top_values(s1, PEER_TOPK)
        v2 = _top_values(s2, PEER_TOPK)
        a1 = s1 - v1[0]
        a2 = s2 - v2[0]
        v1 = [v - v1[0] for v in v1]
        v2 = [v - v2[0] for v in v2]
        best = candidates(v1, v2)
        z = best[0] * 0.0
        for bv in best:
            z = z + jnp.exp(bv)
        log_z = jnp.log(z)
        a1_ref[h] = a1
        a2_ref[h] = a2 - log_z
        v2 = [v - log_z for v in v2]
        thr_ref[pl.ds(h, 1), :] = candidates(v1, v2)[-1]
        return carry

    lax.fori_loop(0, PEER_HEADS, body, 0)


def _peer_scores(xT, wqT, keys1, keys2):
    d, n = xT.shape
    dq = wqT.shape[0]
    tn = _tile(n, 256)
    kshape = (PEER_HEADS, PEER_NKEYS, n)
    kspec = pl.BlockSpec((PEER_HEADS, PEER_NKEYS, tn), lambda i: (0, 0, i))
    wkey = pl.BlockSpec(keys1.shape, lambda i: (0, 0, 0))
    return pl.pallas_call(
        _peer_score_kernel,
        grid=(n // tn,),
        in_specs=[pl.BlockSpec((d, tn), lambda i: (0, i)),
                  pl.BlockSpec((dq, d), lambda i: (0, 0)), wkey, wkey],
        out_specs=[kspec, kspec, pl.BlockSpec((PEER_HEADS, tn), lambda i: (0, i))],
        out_shape=[jax.ShapeDtypeStruct(kshape, F32), jax.ShapeDtypeStruct(kshape, F32),
                   jax.ShapeDtypeStruct((PEER_HEADS, n), F32)],
        scratch_shapes=[pltpu.VMEM((dq, tn), F32), pltpu.VMEM((_PEER_CAND_ROWS, tn), F32)],
        compiler_params=_params("parallel"),
        name="peer_scores",
    )(xT, wqT, keys1, keys2)


def _peer_expert_kernel(xT_ref, u_ref, vT_ref, a1_ref, a2_ref, thr_ref, yT_ref, w_scr):
    j = pl.program_id(1)

    @pl.when(j == 0)
    def _():
        yT_ref[...] = jnp.zeros_like(yT_ref)

    te, tm = w_scr.shape
    rows_per_step = te // PEER_NKEYS
    hid = _dot(u_ref[...], xT_ref[...])
    assert rows_per_step % 8 == 0
    for r in range(rows_per_step):
        group = pl.ds(pl.multiple_of(j * rows_per_step + (r // 8) * 8, 8), 8)
        for c in range(tm // HD):
            cs = slice(c * HD, (c + 1) * HD)
            gate = jnp.zeros((PEER_NKEYS, HD), F32)
            for h in range(PEER_HEADS):
                s = a1_ref[h, group, cs][r % 8:r % 8 + 1, :] + a2_ref[h, :, cs]
                gate = gate + jnp.where(s >= thr_ref[h:h + 1, cs], jnp.exp(s), 0.0)
            hv = hid[r * PEER_NKEYS:(r + 1) * PEER_NKEYS, cs]
            act = 0.5 * hv * (1.0 + lax.erf(hv * (2.0 ** -0.5)))
            w_scr[r * PEER_NKEYS:(r + 1) * PEER_NKEYS, cs] = (gate * act).astype(BF16)
    yT_ref[...] += _dot(vT_ref[...], w_scr[...])


def _peer_experts(xT, u, vT, a1, a2, thr):
    d, n = xT.shape
    e = u.shape[0]
    tm = _tile(n, 512)
    te = _tile(e, 8 * PEER_NKEYS)
    kspec = pl.BlockSpec((PEER_HEADS, PEER_NKEYS, tm), lambda i, j: (0, 0, i))
    return pl.pallas_call(
        _peer_expert_kernel,
        grid=(n // tm, e // te),
        in_specs=[pl.BlockSpec((d, tm), lambda i, j: (0, i)),
                  pl.BlockSpec((te, d), lambda i, j: (j, 0)),
                  pl.BlockSpec((d, te), lambda i, j: (0, j)),
                  kspec, kspec,
                  pl.BlockSpec((PEER_HEADS, tm), lambda i, j: (0, i))],
        out_specs=pl.BlockSpec((d, tm), lambda i, j: (0, i)),
        out_shape=jax.ShapeDtypeStruct((d, n), F32),
        scratch_shapes=[pltpu.VMEM((te, tm), BF16)],
        compiler_params=_params("parallel", "arbitrary"),
        name="peer_experts",
    )(xT, u, vT, a1, a2, thr)


def _peer_ffn(x, xb, w_query, sub_keys, u, v, g, b):
    xT = xb.T
    a1, a2, thr = _peer_scores(xT, w_query.T.astype(BF16), sub_keys[0], sub_keys[1])
    yT = _peer_experts(xT, u.astype(BF16), v.T.astype(BF16), a1, a2, thr)
    return _add_ln(yT.T, x, g, b)


def _rope_proj_kernel(x_ref, w_ref, cos_ref, sin_ref, o_ref, *maybe_mean_ref):
    y = _dot(x_ref[...], w_ref[...])
    tm = y.shape[0]
    cos = cos_ref[...]
    sin = sin_ref[...]
    for h in range(o_ref.shape[0]):
        yh = y[:, h * HD:(h + 1) * HD]
        out = yh * cos + pltpu.roll(yh, HD // 2, axis=1) * sin
        o_ref[h] = out.astype(o_ref.dtype)
        if maybe_mean_ref:
            (mean_ref,) = maybe_mean_ref
            for blk in range(tm // MOBA_BLOCK):
                part = out[blk * MOBA_BLOCK:(blk + 1) * MOBA_BLOCK, :]
                mean_ref[h, blk:blk + 1, :] = jnp.mean(part, axis=0, keepdims=True)


def _rope_proj(xb, w, cos, sin, seq, out_dtype, want_mean):
    n, d = xb.shape
    dout = w.shape[1]
    tm = _tile(seq, 2048)
    tn = _tile(dout, 256)
    nh = dout // HD
    hp = tn // HD
    nt = seq // tm
    tab = pl.BlockSpec((tm, HD), lambda i, j: (i % nt, 0))
    out_specs = [pl.BlockSpec((hp, tm, HD), lambda i, j: (j, i, 0))]
    out_shape = [jax.ShapeDtypeStruct((nh, n, HD), out_dtype)]
    if want_mean:
        nblk = tm // MOBA_BLOCK
        out_specs.append(pl.BlockSpec((hp, nblk, HD), lambda i, j: (j, i, 0)))
        out_shape.append(jax.ShapeDtypeStruct((nh, n // MOBA_BLOCK, HD), F32))
    return pl.pallas_call(
        _rope_proj_kernel,
        grid=(n // tm, dout // tn),
        in_specs=[pl.BlockSpec((tm, d), lambda i, j: (i, 0)),
                  pl.BlockSpec((d, tn), lambda i, j: (0, j)), tab, tab],
        out_specs=out_specs,
        out_shape=out_shape,
        compiler_params=_params("parallel", "parallel"),
        name="rope_proj_mean" if want_mean else "rope_proj",
    )(xb, w, cos, sin)


def _vt_proj_kernel(w_ref, xT_ref, o_ref):
    y = _dot(w_ref[...], xT_ref[...]).astype(o_ref.dtype)
    heads, blocks = o_ref.shape[:2]
    for h in range(heads):
        for blk in range(blocks):
            o_ref[h, blk] = y[h * HD:(h + 1) * HD, blk * MOBA_BLOCK:(blk + 1) * MOBA_BLOCK]


def _vt_proj(wT, xT):
    dout, d = wT.shape
    n = xT.shape[1]
    tm = _tile(dout, 512)
    tn = _tile(n, 1024)
    return pl.pallas_call(
        _vt_proj_kernel,
        grid=(dout // tm, n // tn),
        in_specs=[pl.BlockSpec((tm, d), lambda i, j: (i, 0)),
                  pl.BlockSpec((d, tn), lambda i, j: (0, j))],
        out_specs=pl.BlockSpec((tm // HD, tn // MOBA_BLOCK, HD, MOBA_BLOCK),
                               lambda i, j: (i, j, 0, 0)),
        out_shape=jax.ShapeDtypeStruct((dout // HD, n // MOBA_BLOCK, HD, MOBA_BLOCK), BF16),
        compiler_params=_params("parallel", "parallel"),
        name="vt_proj",
    )(wT, xT)


def _moba_kernel(q_ref, k_ref, vT_ref, km_ref, o_ref, sel_scr):
    i = pl.program_id(2)
    blk = MOBA_BLOCK
    nb = km_ref.shape[0]
    q = q_ref[...]
    qs = (q * (HD ** -0.5)).astype(BF16)

    gate = _dot3(km_ref[...], q, dot=_dot_nt)
    past = lax.broadcasted_iota(jnp.int32, (nb, blk), 0) < i
    gate = jnp.where(past, gate, NEG)
    third = _top_values(gate, MOBA_TOPK)[-1]
    sel_scr[...] = jnp.where(jnp.logical_and(past, gate >= third), 1.0, 0.0)

    own = pl.ds(pl.multiple_of(i * blk, blk), blk)
    s = _dot_nt(k_ref[own, :], qs)
    kpos = lax.broadcasted_iota(jnp.int32, (blk, blk), 0)
    qpos = lax.broadcasted_iota(jnp.int32, (blk, blk), 1)
    s = jnp.where(kpos <= qpos, s, NEG)
    m = jnp.max(s, axis=0, keepdims=True)
    p = jnp.exp(s - m)
    l = jnp.sum(p, axis=0, keepdims=True)
    acc = _dot(vT_ref[i], p.astype(BF16))

    def body(j, carry):
        m, l, acc = carry
        rows = pl.ds(pl.multiple_of(j * blk, blk), blk)
        s = _dot_nt(k_ref[rows, :], qs)
        flag = sel_scr[pl.ds(j, 1), :]
        s = jnp.where(flag > 0.5, s, NEG)
        m_new = jnp.maximum(m, jnp.max(s, axis=0, keepdims=True))
        scale = jnp.exp(m - m_new)
        p = jnp.exp(s - m_new)
        l = scale * l + jnp.sum(p, axis=0, keepdims=True)
        acc = scale * acc + _dot(vT_ref[j], p.astype(BF16))
        return m_new, l, acc

    m, l, acc = lax.fori_loop(0, i, body, (m, l, acc))
    o_ref[...] = (acc / l).T.astype(o_ref.dtype)


def _moba(q, k, vT, k_mean, batch):
    nh, n, _ = q.shape
    t = n // batch
    nb = t // MOBA_BLOCK
    return pl.pallas_call(
        _moba_kernel,
        grid=(nh, batch, nb),
        in_specs=[pl.BlockSpec((None, MOBA_BLOCK, HD), lambda h, b, i: (h, b * nb + i, 0)),
                  pl.BlockSpec((None, t, HD), lambda h, b, i: (h, b, 0)),
                  pl.BlockSpec((None, nb, HD, MOBA_BLOCK), lambda h, b, i: (h, b, 0, 0)),
                  pl.BlockSpec((None, nb, HD), lambda h, b, i: (h, b, 0))],
        out_specs=pl.BlockSpec((MOBA_BLOCK, HD), lambda h, b, i: (b * nb + i, h)),
        out_shape=jax.ShapeDtypeStruct((n, nh * HD), BF16),
        scratch_shapes=[pltpu.VMEM((nb, MOBA_BLOCK), F32)],
        compiler_params=_params("parallel", "parallel", "arbitrary"),
        name="moba",
    )(q, k, vT, k_mean)


def _rope_tables(seq):
    half = HD // 2
    inv = ROPE_THETA ** (-jnp.arange(half, dtype=F32) / half)
    ang = jnp.arange(seq).astype(F32)[:, None] * inv[None, :]
    cos = jnp.cos(ang)
    sin = jnp.sin(ang)
    return jnp.concatenate([cos, cos], axis=1), jnp.concatenate([-sin, sin], axis=1)


def kernel(x, hg_w_in, hg_norm_g, hg_w_out, hg_lb_logits, kv_w, moba_w_q, moba_w_out,
           peer_w_query, peer_sub_keys, peer_u, peer_v, ln_g, ln_b):
    batch, seq, d = x.shape
    n = batch * seq
    assert d % HD == 0 and seq % MOBA_BLOCK == 0
    x0 = x.reshape(n, d)
    lower = jnp.cumsum(jax.nn.softmax(hg_lb_logits.astype(F32), axis=0), axis=0)
    vec = lambda a: a.reshape(1, -1)

    q, lf, k, iv, g = _hg_in(x0.astype(BF16), hg_w_in[0].astype(BF16), vec(lower[0]))
    o = _hg_scan(q, lf, k, iv, g, vec(hg_norm_g[0]), batch)
    x1, x1b = _proj_ln(o, hg_w_out[0].astype(BF16), x0, vec(ln_g[0, 0]), vec(ln_b[0, 0]))
    x2, x2b = _peer_ffn(x1, x1b, peer_w_query[0], peer_sub_keys[0], peer_u[0], peer_v[0],
                        vec(ln_g[0, 1]), vec(ln_b[0, 1]))

    cos, sin = _rope_tables(seq)
    kr, k_mean = _rope_proj(x2b, kv_w[:, :d].astype(BF16), cos, sin, seq, BF16, True)
    vT = _vt_proj(kv_w[:, d:].T.astype(BF16), x2b.T)

    (qr,) = _rope_proj(x2b, moba_w_q[0].astype(BF16), cos, sin, seq, F32, False)
    att = _moba(qr, kr, vT, k_mean, batch)
    x3, x3b = _proj_ln(att, moba_w_out[0].astype(BF16), x2, vec(ln_g[1, 0]), vec(ln_b[1, 0]))
    x4, _ = _peer_ffn(x3, x3b, peer_w_query[1], peer_sub_keys[1], peer_u[1], peer_v[1],
                      vec(ln_g[1, 1]), vec(ln_b[1, 1]))
    return x4.reshape(batch, seq, d)
```

```python
import functools
import math

import jax
import jax.numpy as jnp
from jax import lax
from jax.experimental import pallas as pl
from jax.experimental.pallas import tpu as pltpu

F32 = jnp.float32
BF16 = jnp.bfloat16

HD = 128
DEPTH = 2
HG_CHUNK = 64
HG_HEADS_PER_STEP = 8
MOBA_BLOCK = 256
MOBA_KV_TILE = 2 * MOBA_BLOCK
MOBA_TOPK = 3
ROPE_THETA = 10000.0
PEER_HEADS = 8
PEER_NKEYS = 128
PEER_TOPK = 16
ALPHA = (2.0 * DEPTH) ** 0.25
LN_EPS = 1e-5
RMS_EPS = 1e-6
NEG = -1e30
LOG2E = math.log2(math.e)
PEER_HIDDEN_SCALE = 2.0 ** -0.5
PEER_GATE_LOG2_SCALE = -0.5
PEER_GATE_ROWS = 32
VMEM_LIMIT = 56 * 1024 * 1024

_PEER_PAIRS = [(a, b) for a in range(PEER_TOPK + 1) for b in range(PEER_TOPK + 1)
               if (a + 1) * (b + 1) <= PEER_TOPK + 1]
_PEER_CAND_ROWS = -(-len(_PEER_PAIRS) // 8) * 8


def _params(*sem, flags=None):
    return pltpu.CompilerParams(dimension_semantics=sem, vmem_limit_bytes=VMEM_LIMIT, flags=flags)


def _dot(a, b):
    return jnp.dot(a, b, preferred_element_type=F32)


def _dot_nt(a, b):
    return lax.dot_general(a, b, (((1,), (1,)), ((), ())), preferred_element_type=F32)


def _dot_tn(a, b):
    return lax.dot_general(a, b, (((0,), (0,)), ((), ())), preferred_element_type=F32)


def _split(a):
    hi = a.astype(BF16)
    lo = (a - hi.astype(F32)).astype(BF16)
    return hi, lo


def _dot3(a, b, dot=_dot):
    ah, al = _split(a)
    bh, bl = _split(b)
    return dot(ah, bh) + (dot(ah, bl) + dot(al, bh))


def _layer_norm(z, g, b):
    mu = jnp.mean(z, axis=-1, keepdims=True)
    zc = z - mu
    var = jnp.mean(zc * zc, axis=-1, keepdims=True)
    return zc * lax.rsqrt(var + LN_EPS) * g + b


def _tile(n, want):
    t = min(n, want)
    assert n % t == 0, (n, want)
    return t


def _hg_in_kernel(x_ref, wq_ref, wf_ref, wi_ref, wg_ref, lb_ref,
                  q_ref, lf_ref, k_ref, i_ref, g_ref):
    x = x_ref[...]
    heads = q_ref.shape[0]

    def put(ref, val):
        for h in range(heads):
            ref[h] = val[:, h * HD:(h + 1) * HD].astype(ref.dtype)

    q = _dot(x, wq_ref[...])
    put(q_ref, q * jax.nn.sigmoid(q))
    lb = lb_ref[...]
    f = lb + (1.0 - lb) * jax.nn.sigmoid(_dot(x, wf_ref[...]))
    put(lf_ref, jnp.log(f))
    put(k_ref, 1.0 - f)
    put(i_ref, _dot(x, wi_ref[...]))
    g = _dot(x, wg_ref[...])
    put(g_ref, g * jax.nn.sigmoid(g))


def _hg_in(xb, w_in, lb):
    n, d = xb.shape
    tm = _tile(n, 1024)
    tn = _tile(d, 256)
    nh = d // HD
    hp = tn // HD
    nj = d // tn
    w_specs = [pl.BlockSpec((d, tn), functools.partial(lambda i, j, g: (0, j + g * nj), g=g))
               for g in range(4)]
    hm = lambda dt: jax.ShapeDtypeStruct((nh, n, HD), dt)
    o_spec = pl.BlockSpec((hp, tm, HD), lambda i, j: (j, i, 0))
    return pl.pallas_call(
        _hg_in_kernel,
        grid=(n // tm, nj),
        in_specs=[pl.BlockSpec((tm, d), lambda i, j: (i, 0))] + w_specs
                 + [pl.BlockSpec((1, tn), lambda i, j: (0, j))],
        out_specs=[o_spec] * 5,
        out_shape=[hm(BF16), hm(F32), hm(BF16), hm(BF16), hm(BF16)],
        compiler_params=_params("parallel", "parallel"),
        name="hg_in",
    )(xb, w_in, w_in, w_in, w_in, lb)


def _hg_scan_kernel(q_ref, lf_ref, k_ref, i_ref, g_ref, gn_ref, o_ref, st_ref, *, chunk):
    @pl.when(pl.program_id(2) == 0)
    def _():
        st_ref[...] = jnp.zeros_like(st_ref)

    heads, tb, _ = q_ref.shape
    row = lax.broadcasted_iota(jnp.int32, (chunk, chunk), 0)
    col = lax.broadcasted_iota(jnp.int32, (chunk, chunk), 1)
    causal = row >= col
    tri = causal.astype(BF16)
    gn = gn_ref[...]

    def body(c, carry):
        r0 = pl.multiple_of(c * chunk, chunk)
        rows = pl.ds(r0, chunk)
        for h in range(heads):
            lf = lf_ref[h, rows, :]
            lf_hi, lf_lo = _split(lf)
            b = _dot(tri, lf_hi) + _dot(tri, lf_lo)
            eb = jnp.exp(b)
            enb = jnp.exp(jnp.minimum(-b, 80.0))
            qd = (q_ref[h, rows, :].astype(F32) * eb).astype(BF16)
            kd = k_ref[h, rows, :].astype(F32) * enb
            e_last = eb[chunk - 1:chunk, :]
            kl = (kd * e_last).astype(BF16)
            iv = i_ref[h, rows, :]
            st = st_ref[h]
            inter = _dot_nt(qd, st.astype(BF16))
            att = jnp.where(causal, _dot_nt(qd, kd.astype(BF16)), 0.0)
            o = inter + _dot(att.astype(BF16), iv)
            st_ref[h] = st * e_last + _dot_tn(iv, kl)
            o = o * lax.rsqrt(jnp.mean(o * o, axis=-1, keepdims=True) + RMS_EPS) * gn
            o = o * g_ref[h, rows, :].astype(F32)
            o_ref[rows, h * HD:(h + 1) * HD] = o.astype(o_ref.dtype)
        return carry

    lax.fori_loop(0, tb // chunk, body, 0)


def _hg_scan(q, lf, k, iv, g, g_norm, batch):
    nh, n, _ = q.shape
    t = n // batch
    tb = _tile(t, 1024)
    hp = math.gcd(nh, HG_HEADS_PER_STEP)
    nt = t // tb
    spec = pl.BlockSpec((hp, tb, HD), lambda h, b, s: (h, b * nt + s, 0))
    return pl.pallas_call(
        functools.partial(_hg_scan_kernel, chunk=HG_CHUNK),
        grid=(nh // hp, batch, nt),
        in_specs=[spec] * 5 + [pl.BlockSpec((1, HD), lambda h, b, s: (0, 0))],
        out_specs=pl.BlockSpec((tb, hp * HD), lambda h, b, s: (b * nt + s, h)),
        out_shape=jax.ShapeDtypeStruct((n, nh * HD), BF16),
        scratch_shapes=[pltpu.VMEM((hp, HD, HD), F32)],
        compiler_params=_params("parallel", "parallel", "arbitrary"),
        name="hg_scan",
    )(q, lf, k, iv, g, g_norm)


def _proj_ln_kernel(a_ref, w_ref, res_ref, g_ref, b_ref, x_ref, xb_ref):
    z = ALPHA * res_ref[...] + _dot(a_ref[...], w_ref[...])
    out = _layer_norm(z, g_ref[...], b_ref[...])
    x_ref[...] = out
    xb_ref[...] = out.astype(BF16)


def _proj_ln(a, w, res, g, b):
    n, d = res.shape
    kd = a.shape[1]
    tm = _tile(n, 256)
    row = pl.BlockSpec((tm, d), lambda i: (i, 0))
    vec = pl.BlockSpec((1, d), lambda i: (0, 0))
    return pl.pallas_call(
        _proj_ln_kernel,
        grid=(n // tm,),
        in_specs=[pl.BlockSpec((tm, kd), lambda i: (i, 0)),
                  pl.BlockSpec((kd, d), lambda i: (0, 0)), row, vec, vec],
        out_specs=[row, row],
        out_shape=[jax.ShapeDtypeStruct((n, d), F32), jax.ShapeDtypeStruct((n, d), BF16)],
        compiler_params=_params("parallel"),
        name="proj_ln",
    )(a, w, res, g, b)


def _add_ln_kernel(y_ref, res_ref, g_ref, b_ref, x_ref, xb_ref):
    out = _layer_norm(ALPHA * res_ref[...] + y_ref[...], g_ref[...], b_ref[...])
    x_ref[...] = out
    xb_ref[...] = out.astype(BF16)


def _add_ln(y, res, g, b):
    n, d = res.shape
    tm = _tile(n, 512)
    row = pl.BlockSpec((tm, d), lambda i: (i, 0))
    vec = pl.BlockSpec((1, d), lambda i: (0, 0))
    return pl.pallas_call(
        _add_ln_kernel,
        grid=(n // tm,),
        in_specs=[row, row, vec, vec],
        out_specs=[row, row],
        out_shape=[jax.ShapeDtypeStruct((n, d), F32), jax.ShapeDtypeStruct((n, d), BF16)],
        compiler_params=_params("parallel"),
        name="add_ln",
    )(y, res, g, b)


def _top_values(s, k):
    vals = []
    for _ in range(k):
        m = jnp.max(s, axis=0, keepdims=True)
        vals.append(m)
        s = jnp.where(s == m, -jnp.inf, s)
    return vals


def _peer_score_kernel(xT_ref, wq_ref, k1_ref, k2_ref, a1_ref, a2_ref, thr_ref,
                       q_scr, cand_scr):
    q_scr[...] = _dot(wq_ref[...], xT_ref[...])
    tn = q_scr.shape[1]
    half = PEER_NKEYS
    pad = _PEER_CAND_ROWS - len(_PEER_PAIRS)

    def candidates(v1, v2):
        for r, (a, b) in enumerate(_PEER_PAIRS):
            cand_scr[r:r + 1, :] = v1[a] + v2[b]
        if pad:
            cand_scr[len(_PEER_PAIRS):, :] = jnp.full((pad, tn), -jnp.inf, F32)
        return _top_values(cand_scr[...], PEER_TOPK + 1)

    def body(h, carry):
        base = pl.multiple_of(h * 2 * half, 2 * half)
        s1 = _dot3(k1_ref[h], q_scr[pl.ds(base, half), :])
        s2 = _dot3(k2_ref[h], q_scr[pl.ds(base + half, half), :])
        v1 = _top_values(s1, PEER_TOPK + 1)
        v2 = _top_values(s2, PEER_TOPK + 1)
        m1, m2 = v1[0], v2[0]
        best = candidates([v - m1 for v in v1], [v - m2 for v in v2])
        z = best[0] * 0.0
        for bv in best[:PEER_TOPK]:
            z = z + jnp.exp(bv)
        log_z = jnp.log(z)
        map1 = lambda v: (v - m1) * LOG2E
        map2 = lambda v: (v - m2 - log_z) * LOG2E + PEER_GATE_LOG2_SCALE
        a1_ref[h] = map1(s1)
        a2_ref[h] = map2(s2)
        best = candidates([map1(v) for v in v1], [map2(v) for v in v2])
        thr_ref[pl.ds(h, 1), :] = 0.5 * (best[PEER_TOPK - 1] + best[PEER_TOPK])
        return carry

    lax.fori_loop(0, PEER_HEADS, body, 0)


def _peer_scores(xT, wqT, keys1, keys2):
    d, n = xT.shape
    dq = wqT.shape[0]
    tn = _tile(n, 256)
    kshape = (PEER_HEADS, PEER_NKEYS, n)
    kspec = pl.BlockSpec((PEER_HEADS, PEER_NKEYS, tn), lambda i: (0, 0, i))
    wkey = pl.BlockSpec(keys1.shape, lambda i: (0, 0, 0))
    return pl.pallas_call(
        _peer_score_kernel,
        grid=(n // tn,),
        in_specs=[pl.BlockSpec((d, tn), lambda i: (0, i)),
                  pl.BlockSpec((dq, d), lambda i: (0, 0)), wkey, wkey],
        out_specs=[kspec, kspec, pl.BlockSpec((PEER_HEADS, tn), lambda i: (0, i))],
        out_shape=[jax.ShapeDtypeStruct(kshape, F32), jax.ShapeDtypeStruct(kshape, F32),
                   jax.ShapeDtypeStruct((PEER_HEADS, n), F32)],
        scratch_shapes=[pltpu.VMEM((dq, tn), F32), pltpu.VMEM((_PEER_CAND_ROWS, tn), F32)],
        compiler_params=_params("parallel"),
        name="peer_scores",
    )(xT, wqT, keys1, keys2)


def _peer_expert_kernel(xT_ref, u_ref, vT_ref, a1_ref, a2_ref, thr_ref, yT_ref, hid_scr, w_scr):
    j = pl.program_id(1)

    @pl.when(j == 0)
    def _():
        yT_ref[...] = jnp.zeros_like(yT_ref)

    te, tm = w_scr.shape
    rows_per_step = te // PEER_NKEYS
    assert rows_per_step % 8 == 0
    hid_scr[...] = _dot(u_ref[...], xT_ref[...])
    for r in range(rows_per_step):
        group = pl.ds(pl.multiple_of(j * rows_per_step + (r // 8) * 8, 8), 8)
        for c in range(tm // HD):
            cs = slice(c * HD, (c + 1) * HD)
            for k in range(PEER_NKEYS // PEER_GATE_ROWS):
                keys = slice(k * PEER_GATE_ROWS, (k + 1) * PEER_GATE_ROWS)
                rows = slice(r * PEER_NKEYS + keys.start, r * PEER_NKEYS + keys.stop)
                gate = jnp.zeros((PEER_GATE_ROWS, HD), F32)
                for h in range(PEER_HEADS):
                    s = a1_ref[h, group, cs][r % 8:r % 8 + 1, :] + a2_ref[h, keys, cs]
                    gate = gate + jnp.where(s >= thr_ref[h:h + 1, cs], jnp.exp2(s), 0.0)
                hv = hid_scr[rows, cs]
                w_scr[rows, cs] = (gate * (hv * (1.0 + lax.erf(hv)))).astype(BF16)
    yT_ref[...] += _dot(vT_ref[...], w_scr[...])


def _peer_experts(xT, u, vT, a1, a2, thr):
    d, n = xT.shape
    e = u.shape[0]
    tm = _tile(n, 512)
    te = _tile(e, 8 * PEER_NKEYS)
    kspec = pl.BlockSpec((PEER_HEADS, PEER_NKEYS, tm), lambda i, j: (0, 0, i))
    return pl.pallas_call(
        _peer_expert_kernel,
        grid=(n // tm, e // te),
        in_specs=[pl.BlockSpec((d, tm), lambda i, j: (0, i)),
                  pl.BlockSpec((te, d), lambda i, j: (j, 0)),
                  pl.BlockSpec((d, te), lambda i, j: (0, j)),
                  kspec, kspec,
                  pl.BlockSpec((PEER_HEADS, tm), lambda i, j: (0, i))],
        out_specs=pl.BlockSpec((d, tm), lambda i, j: (0, i)),
        out_shape=jax.ShapeDtypeStruct((d, n), F32),
        scratch_shapes=[pltpu.VMEM((te, tm), F32), pltpu.VMEM((te, tm), BF16)],
        compiler_params=_params("parallel", "arbitrary"),
        name="peer_experts",
    )(xT, u, vT, a1, a2, thr)


def _peer_ffn(x, xb, w_query, sub_keys, u, v, g, b):
    xT = xb.T
    a1, a2, thr = _peer_scores(xT, w_query.T.astype(BF16), sub_keys[0], sub_keys[1])
    yT = _peer_experts(xT, (u * PEER_HIDDEN_SCALE).astype(BF16), v.T.astype(BF16), a1, a2, thr)
    return _add_ln(yT.T, x, g, b)


def _rope_proj_kernel(x_ref, w_ref, cos_ref, sin_ref, o_ref, *maybe_mean_ref):
    y = _dot(x_ref[...], w_ref[...])
    tm = y.shape[0]
    cos = cos_ref[...]
    sin = sin_ref[...]
    for h in range(o_ref.shape[0]):
        yh = y[:, h * HD:(h + 1) * HD]
        out = yh * cos + pltpu.roll(yh, HD // 2, axis=1) * sin
        o_ref[h] = out.astype(o_ref.dtype)
        if maybe_mean_ref:
            (mean_ref,) = maybe_mean_ref
            for blk in range(tm // MOBA_BLOCK):
                part = out[blk * MOBA_BLOCK:(blk + 1) * MOBA_BLOCK, :]
                mean_ref[h, blk:blk + 1, :] = jnp.mean(part, axis=0, keepdims=True)


def _rope_proj(xb, w, cos, sin, seq, out_dtype, want_mean):
    n, d = xb.shape
    dout = w.shape[1]
    tm = _tile(seq, 2048)
    tn = _tile(dout, 256)
    nh = dout // HD
    hp = tn // HD
    nt = seq // tm
    tab = pl.BlockSpec((tm, HD), lambda i, j: (i % nt, 0))
    out_specs = [pl.BlockSpec((hp, tm, HD), lambda i, j: (j, i, 0))]
    out_shape = [jax.ShapeDtypeStruct((nh, n, HD), out_dtype)]
    if want_mean:
        nblk = tm // MOBA_BLOCK
        out_specs.append(pl.BlockSpec((hp, nblk, HD), lambda i, j: (j, i, 0)))
        out_shape.append(jax.ShapeDtypeStruct((nh, n // MOBA_BLOCK, HD), F32))
    return pl.pallas_call(
        _rope_proj_kernel,
        grid=(n // tm, dout // tn),
        in_specs=[pl.BlockSpec((tm, d), lambda i, j: (i, 0)),
                  pl.BlockSpec((d, tn), lambda i, j: (0, j)), tab, tab],
        out_specs=out_specs,
        out_shape=out_shape,
        compiler_params=_params("parallel", "parallel"),
        name="rope_proj_mean" if want_mean else "rope_proj",
    )(xb, w, cos, sin)


def _vt_proj_kernel(w_ref, xT_ref, o_ref):
    y = _dot(w_ref[...], xT_ref[...]).astype(o_ref.dtype)
    heads, blocks, _, width = o_ref.shape
    for h in range(heads):
        for blk in range(blocks):
            o_ref[h, blk] = y[h * HD:(h + 1) * HD, blk * width:(blk + 1) * width]


def _vt_proj(wT, xT):
    dout, d = wT.shape
    n = xT.shape[1]
    tm = _tile(dout, 512)
    tn = _tile(n, 1024)
    return pl.pallas_call(
        _vt_proj_kernel,
        grid=(dout // tm, n // tn),
        in_specs=[pl.BlockSpec((tm, d), lambda i, j: (i, 0)),
                  pl.BlockSpec((d, tn), lambda i, j: (0, j))],
        out_specs=pl.BlockSpec((tm // HD, tn // MOBA_KV_TILE, HD, MOBA_KV_TILE),
                               lambda i, j: (i, j, 0, 0)),
        out_shape=jax.ShapeDtypeStruct((dout // HD, n // MOBA_KV_TILE, HD, MOBA_KV_TILE), BF16),
        compiler_params=_params("parallel", "parallel"),
        name="vt_proj",
    )(wT, xT)


def _moba_kernel(q_ref, k_ref, vT_ref, km_ref, o_ref, sel_scr, s_scr):
    i = pl.program_id(2)
    heads = q_ref.shape[0]
    blk, kt = MOBA_BLOCK, MOBA_KV_TILE
    nb = km_ref.shape[1]
    own_tile = i // 2

    kpos = lax.broadcasted_iota(jnp.int32, (kt, blk), 0) - (i % 2) * blk
    qpos = lax.broadcasted_iota(jnp.int32, (kt, blk), 1)
    causal = kpos <= qpos
    is_prev = kpos < 0
    past = lax.broadcasted_iota(jnp.int32, (nb, blk), 0) < i

    qs = []
    for h in range(heads):
        q = q_ref[h]
        qs.append((q * (HD ** -0.5 * LOG2E)).astype(BF16))
        gate = jnp.where(past, _dot3(km_ref[h], q, dot=_dot_nt), NEG)
        third = _top_values(gate, MOBA_TOPK)[-1]
        sel_scr[h] = jnp.where(jnp.logical_and(past, gate >= third), 1.0, 0.0)

        rows = pl.ds(pl.multiple_of(own_tile * kt, kt), kt)
        s = _dot_nt(k_ref[h, rows, :], qs[h])
        prev_ok = sel_scr[h, pl.ds(2 * own_tile, 1), :] > 0.5
        ok = jnp.logical_and(causal, jnp.logical_or(jnp.logical_not(is_prev), prev_ok))
        s_scr[0, h] = jnp.where(ok, s, NEG)

    def body(u, carry):
        slot = u % 2
        tile = jnp.minimum(u, own_tile - 1)
        tile = jnp.maximum(tile, 0)
        rows = pl.ds(pl.multiple_of(tile * kt, kt), kt)
        out = []
        for h in range(heads):
            m, l, acc = carry[h]
            s0 = s_scr[slot, h, :blk, :]
            s1 = s_scr[slot, h, blk:, :]
            m_new = jnp.maximum(m, jnp.maximum(jnp.max(s0, axis=0, keepdims=True),
                                               jnp.max(s1, axis=0, keepdims=True)))
            p0 = jnp.exp2(s0 - m_new)
            p1 = jnp.exp2(s1 - m_new)
            part_l = jnp.sum(p0, axis=0, keepdims=True) + jnp.sum(p1, axis=0, keepdims=True)
            vt = vT_ref[h, jnp.where(u == 0, own_tile, u - 1)]
            part = _dot(vt[:, :blk], p0.astype(BF16)) + _dot(vt[:, blk:], p1.astype(BF16))
            scale = jnp.exp2(m - m_new)
            out.append((m_new, scale * l + part_l, scale * acc + part))

            s = _dot_nt(k_ref[h, rows, :], qs[h])
            s_scr[1 - slot, h, :blk, :] = jnp.where(sel_scr[h, pl.ds(2 * tile, 1), :] > 0.5, s[:blk], NEG)
            s_scr[1 - slot, h, blk:, :] = jnp.where(sel_scr[h, pl.ds(2 * tile + 1, 1), :] > 0.5, s[blk:], NEG)
        return tuple(out)

    init = tuple((jnp.full((1, blk), NEG, F32), jnp.zeros((1, blk), F32), jnp.zeros((HD, blk), F32))
                 for _ in range(heads))
    state = lax.fori_loop(0, own_tile + 1, body, init)
    for h in range(heads):
        _, l, acc = state[h]
        o_ref[:, h * HD:(h + 1) * HD] = (acc / l).T.astype(o_ref.dtype)


def _moba(q, k, vT, k_mean, batch):
    nh, n, _ = q.shape
    t = n // batch
    assert t % MOBA_KV_TILE == 0
    nb = t // MOBA_BLOCK
    hp = 2 if nh % 2 == 0 else 1
    return pl.pallas_call(
        _moba_kernel,
        grid=(nh // hp, batch, nb),
        in_specs=[pl.BlockSpec((hp, MOBA_BLOCK, HD), lambda h, b, i: (h, b * nb + i, 0)),
                  pl.BlockSpec((hp, t, HD), lambda h, b, i: (h, b, 0)),
                  pl.BlockSpec((hp, t // MOBA_KV_TILE, HD, MOBA_KV_TILE), lambda h, b, i: (h, b, 0, 0)),
                  pl.BlockSpec((hp, nb, HD), lambda h, b, i: (h, b, 0))],
        out_specs=pl.BlockSpec((MOBA_BLOCK, hp * HD), lambda h, b, i: (b * nb + i, h)),
        out_shape=jax.ShapeDtypeStruct((n, nh * HD), BF16),
        scratch_shapes=[pltpu.VMEM((hp, nb, MOBA_BLOCK), F32),
                        pltpu.VMEM((2, hp, MOBA_KV_TILE, MOBA_BLOCK), F32)],
        compiler_params=_params("parallel", "parallel", "arbitrary"),
        name="moba",
    )(q, k, vT, k_mean)


def _rope_tables(seq):
    half = HD // 2
    inv = ROPE_THETA ** (-jnp.arange(half, dtype=F32) / half)
    ang = jnp.arange(seq).astype(F32)[:, None] * inv[None, :]
    cos = jnp.cos(ang)
    sin = jnp.sin(ang)
    return jnp.concatenate([cos, cos], axis=1), jnp.concatenate([-sin, sin], axis=1)


def kernel(x, hg_w_in, hg_norm_g, hg_w_out, hg_lb_logits, kv_w, moba_w_q, moba_w_out,
           peer_w_query, peer_sub_keys, peer_u, peer_v, ln_g, ln_b):
    batch, seq, d = x.shape
    n = batch * seq
    assert d % HD == 0 and seq % MOBA_BLOCK == 0
    x0 = x.reshape(n, d)
    lower = jnp.cumsum(jax.nn.softmax(hg_lb_logits.astype(F32), axis=0), axis=0)
    vec = lambda a: a.reshape(1, -1)

    q, lf, k, iv, g = _hg_in(x0.astype(BF16), hg_w_in[0].astype(BF16), vec(lower[0]))
    o = _hg_scan(q, lf, k, iv, g, vec(hg_norm_g[0]), batch)
    x1, x1b = _proj_ln(o, hg_w_out[0].astype(BF16), x0, vec(ln_g[0, 0]), vec(ln_b[0, 0]))
    x2, x2b = _peer_ffn(x1, x1b, peer_w_query[0], peer_sub_keys[0], peer_u[0], peer_v[0],
                        vec(ln_g[0, 1]), vec(ln_b[0, 1]))

    cos, sin = _rope_tables(seq)
    kr, k_mean = _rope_proj(x2b, kv_w[:, :d].astype(BF16), cos, sin, seq, BF16, True)
    vT = _vt_proj(kv_w[:, d:].T.astype(BF16), x2b.T)

    (qr,) = _rope_proj(x2b, moba_w_q[0].astype(BF16), cos, sin, seq, F32, False)
    att = _moba(qr, kr, vT, k_mean, batch)
    x3, x3b = _proj_ln(att, moba_w_out[0].astype(BF16), x2, vec(ln_g[1, 0]), vec(ln_b[1, 0]))
    x4, _ = _peer_ffn(x3, x3b, peer_w_query[1], peer_sub_keys[1], peer_u[1], peer_v[1],
                      vec(ln_g[1, 1]), vec(ln_b[1, 1]))
    return x4.reshape(batch, seq, d)
```
